```python
import math
import jax, jax.numpy as jnp
from jax import lax
import numpy as np

D_MODEL = 1024
BATCH = 8
SEQ = 2048
DEPTH = 2
DEC_BATCH = 32
DEC_SEQ = 2048
PAST_LEN = 128

N_MIXERS = 2
N_A_LAYERS = (DEPTH + 1) // 2
N_B_LAYERS = DEPTH // 2
N_HEADS = 4
HEAD_DIM = D_MODEL // N_HEADS
CHUNK = 128
N_GATE_COLS = 4 * N_HEADS
M_INIT = -1e30
HEAD_NORM_EPS = 1e-6
CONV_W = 3
N_EXPERTS = 16
N_GROUPS = 4
EXPERTS_PER_GROUP = N_EXPERTS // N_GROUPS
TOP_K = 2
D_EXPERT = 256
LN_EPS = 1e-5
ALPHA = (2.0 * DEPTH) ** 0.25
BETA = (8.0 * DEPTH) ** -0.25

kernel_name = "hybrid_mlstm_shortconv_grouped_moe_encoder"

F32 = jnp.float32


def _layer_norm(x, g, b):
    xf = x.astype(F32)
    mu = jnp.mean(xf, axis=-1, keepdims=True)
    var = jnp.mean(jnp.square(xf - mu), axis=-1, keepdims=True)
    y = (xf - mu) * lax.rsqrt(var + LN_EPS) * g.astype(F32) + b.astype(F32)
    return y.astype(x.dtype)


def _mlstm_one_direction(q, k, v, log_i, log_f):
    bsz, nh, seq, dh = q.shape
    nc = seq // CHUNK

    def chunks(a):
        a = a.reshape((bsz, nh, nc, CHUNK) + a.shape[3:])
        return jnp.moveaxis(a, 2, 0)

    tril = jnp.tril(jnp.ones((CHUNK, CHUNK), dtype=bool))

    def step(carry, xs):
        c_prev, n_prev, m_prev = carry
        qb, kb, vb, ib, fb = xs
        f_cum = jnp.cumsum(fb, axis=-1)
        d_log = f_cum[..., :, None] - f_cum[..., None, :] + ib[..., None, :]
        d_log = jnp.where(tril, d_log, -jnp.inf)
        inter = f_cum + m_prev[..., None]
        m_row = jnp.maximum(jnp.max(d_log, axis=-1), inter)
        w = jnp.exp(d_log - m_row[..., None])
        s = jnp.einsum("bhid,bhjd->bhij", qb, kb) * w
        a_inter = jnp.exp(inter - m_row)
        num = jnp.einsum("bhij,bhjd->bhid", s, vb) + a_inter[..., None] * jnp.einsum("bhid,bhde->bhie", qb, c_prev)
        den = jnp.sum(s, axis=-1) + a_inter * jnp.einsum("bhid,bhd->bhi", qb, n_prev)
        h = num / jnp.maximum(jnp.abs(den), jnp.exp(-m_row))[..., None]
        f_tot = f_cum[..., -1]
        g = f_tot[..., None] - f_cum + ib
        m_new = jnp.maximum(f_tot + m_prev, jnp.max(g, axis=-1))
        wk = jnp.exp(g - m_new[..., None])
        decay = jnp.exp(f_tot + m_prev - m_new)
        c_new = decay[..., None, None] * c_prev + jnp.einsum("bhjd,bhje->bhde", kb * wk[..., None], vb)
        n_new = decay[..., None] * n_prev + jnp.einsum("bhj,bhjd->bhd", wk, kb)
        return (c_new, n_new, m_new), h

    init = (jnp.zeros((bsz, nh, dh, dh), F32), jnp.zeros((bsz, nh, dh), F32), jnp.full((bsz, nh), M_INIT, F32))
    _, h = lax.scan(step, init, (chunks(q), chunks(k), chunks(v), chunks(log_i), chunks(log_f)))
    return jnp.moveaxis(h, 0, 2).reshape(bsz, nh, seq, dh)


def _mlstm_mixer(x, w_in, b_gate, norm_g, w_out):
    bsz, seq, _ = x.shape
    proj = x @ w_in
    q, k, v, o = (proj[..., j * D_MODEL:(j + 1) * D_MODEL] for j in range(4))
    gates = proj[..., 4 * D_MODEL:].astype(F32) + b_gate.astype(F32)
    gates = gates.reshape(bsz, seq, 4, N_HEADS).transpose(2, 0, 3, 1)
    i_fw, f_fw, i_bw, f_bw = gates[0], gates[1], gates[2], gates[3]

    def heads(a):
        return a.reshape(bsz, seq, N_HEADS, HEAD_DIM).transpose(0, 2, 1, 3).astype(F32)

    qh = heads(q)
    kh = heads(k) * (HEAD_DIM ** -0.5)
    vh = heads(v)
    h_fw = _mlstm_one_direction(qh, kh, vh, i_fw, jax.nn.log_sigmoid(f_fw))
    flip = lambda a: jnp.flip(a, axis=2)
    h_bw = flip(_mlstm_one_direction(flip(qh), flip(kh), flip(vh), flip(i_bw), flip(jax.nn.log_sigmoid(f_bw))))
    h = h_fw + h_bw
    mu = jnp.mean(h, axis=-1, keepdims=True)
    var = jnp.mean(jnp.square(h - mu), axis=-1, keepdims=True)
    hn = (h - mu) * lax.rsqrt(var + HEAD_NORM_EPS)
    hn = hn.transpose(0, 2, 1, 3).reshape(bsz, seq, D_MODEL) * norm_g.astype(F32)
    y = (jax.nn.sigmoid(o.astype(F32)) * hn).astype(x.dtype)
    return y @ w_out


def _shortconv_mixer(x, w_in, conv_w, w_out):
    proj = x @ w_in
    b_gate = proj[..., :D_MODEL]
    c_gate = proj[..., D_MODEL:2 * D_MODEL]
    hv = proj[..., 2 * D_MODEL:]
    u = c_gate * hv
    kern = conv_w.astype(u.dtype)[:, None, :]
    conv = lax.conv_general_dilated(u, kern, window_strides=(1,), padding="SAME",
                                    dimension_numbers=("NWC", "WIO", "NWC"),
                                    feature_group_count=D_MODEL)
    return (b_gate * conv) @ w_out


def _grouped_moe(x, router_w, router_b, w_gu, w_down):
    bsz, seq, dm = x.shape
    t = x.reshape(-1, dm)
    logits = (t @ router_w).astype(F32) + router_b.astype(F32)
    probs = jax.nn.softmax(logits, axis=-1)
    grouped = probs.reshape(-1, N_GROUPS, EXPERTS_PER_GROUP)
    group_score = jnp.sum(lax.top_k(grouped, TOP_K)[0], axis=-1)
    g_sel = jnp.argmax(group_score, axis=-1)
    in_group = (jnp.arange(N_EXPERTS) // EXPERTS_PER_GROUP)[None, :] == g_sel[:, None]
    top_v, top_i = lax.top_k(jnp.where(in_group, probs, -1.0), TOP_K)
    gates = top_v / jnp.sum(top_v, axis=-1, keepdims=True)
    combine = jnp.sum(jax.nn.one_hot(top_i, N_EXPERTS, dtype=F32) * gates[..., None], axis=1)
    out = jnp.zeros(t.shape, F32)
    for e in range(N_EXPERTS):
        gu = t @ w_gu[e]
        h = jax.nn.silu(gu[:, :D_EXPERT]) * gu[:, D_EXPERT:]
        out = out + combine[:, e:e + 1] * (h @ w_down[e]).astype(F32)
    return out.astype(x.dtype).reshape(bsz, seq, dm)


def _trunk(x, router_w, router_b, mlstm_w_in, mlstm_b_gate, mlstm_norm_g, mlstm_w_out,
           conv_w_in, conv_w, conv_w_out, ln1_g, ln1_b, moe_w_gu, moe_w_down, ln2_g, ln2_b):
    for i in range(DEPTH):
        j = i // N_MIXERS
        if i % N_MIXERS == 0:
            mix = _mlstm_mixer(x, mlstm_w_in[j], mlstm_b_gate[j], mlstm_norm_g[j], mlstm_w_out[j])
        else:
            mix = _shortconv_mixer(x, conv_w_in[j], conv_w[j], conv_w_out[j])
        x = _layer_norm(ALPHA * x + mix, ln1_g[i], ln1_b[i])
        ffn = _grouped_moe(x, router_w, router_b, moe_w_gu[i], moe_w_down[i])
        x = _layer_norm(ALPHA * x + ffn, ln2_g[i], ln2_b[i])
    return x


def _normal(k, shape, scale):
    return jax.random.normal(k, shape, F32) * scale


def setup_inputs(seed: int = 0) -> dict:
    key = jax.random.key(seed)
    ks = jax.random.split(key, 20)
    d = D_MODEL
    i_bias = _normal(ks[5], (N_A_LAYERS, 2, N_HEADS), 0.1)
    f_bias = jnp.linspace(3.0, 6.0, N_HEADS, dtype=F32)[None, None, :] + _normal(ks[6], (N_A_LAYERS, 2, N_HEADS), 0.1)
    b_gate = jnp.stack([i_bias, f_bias], axis=2).reshape(N_A_LAYERS, N_GATE_COLS)
    return {
        "x_prompt": _normal(ks[0], (BATCH, SEQ, d), 1.0),
        "x_sample": _normal(ks[1], (DEC_BATCH, DEC_SEQ, d), 1.0),
        "router_w": _normal(ks[2], (d, N_EXPERTS), d ** -0.5),
        "router_b": _normal(ks[3], (N_EXPERTS,), 0.01),
        "mlstm_w_in": _normal(ks[4], (N_A_LAYERS, d, 4 * d + N_GATE_COLS), d ** -0.5),
        "mlstm_b_gate": b_gate,
        "mlstm_norm_g": 1.0 + _normal(ks[7], (N_A_LAYERS, d), 0.02),
        "mlstm_w_out": _normal(ks[8], (N_A_LAYERS, d, d), BETA * d ** -0.5),
        "conv_w_in": _normal(ks[9], (N_B_LAYERS, d, 3 * d), d ** -0.5),
        "conv_w": _normal(ks[10], (N_B_LAYERS, CONV_W, d), CONV_W ** -0.5),
        "conv_w_out": _normal(ks[11], (N_B_LAYERS, d, d), BETA * d ** -0.5),
        "ln1_g": 1.0 + _normal(ks[12], (DEPTH, d), 0.02),
        "ln1_b": _normal(ks[13], (DEPTH, d), 0.02),
        "moe_w_gu": _normal(ks[14], (DEPTH, N_EXPERTS, d, 2 * D_EXPERT), d ** -0.5),
        "moe_w_down": _normal(ks[15], (DEPTH, N_EXPERTS, D_EXPERT, d), BETA * D_EXPERT ** -0.5),
        "ln2_g": 1.0 + _normal(ks[16], (DEPTH, d), 0.02),
        "ln2_b": _normal(ks[17], (DEPTH, d), 0.02),
    }


def reference(x_prompt, x_sample, router_w, router_b, mlstm_w_in, mlstm_b_gate, mlstm_norm_g, mlstm_w_out,
              conv_w_in, conv_w, conv_w_out, ln1_g, ln1_b, moe_w_gu, moe_w_down, ln2_g, ln2_b):
    y_prompt = _trunk(x_prompt, router_w, router_b, mlstm_w_in, mlstm_b_gate, mlstm_norm_g, mlstm_w_out,
                      conv_w_in, conv_w, conv_w_out, ln1_g, ln1_b, moe_w_gu, moe_w_down, ln2_g, ln2_b)
    y_sample = _trunk(x_sample, router_w, router_b, mlstm_w_in, mlstm_b_gate, mlstm_norm_g, mlstm_w_out,
                      conv_w_in, conv_w, conv_w_out, ln1_g, ln1_b, moe_w_gu, moe_w_down, ln2_g, ln2_b)
    return (y_prompt, y_sample)
```

```python
import functools

import jax
import jax.numpy as jnp
from jax import lax
from jax.experimental import pallas as pl
from jax.experimental.pallas import tpu as pltpu

F32 = jnp.float32
BF16 = jnp.bfloat16

D_MODEL = 1024
DEPTH = 2
N_HEADS = 4
HEAD_DIM = D_MODEL // N_HEADS
CHUNK = 128
N_GATE_COLS = 4 * N_HEADS
M_INIT = -1e30
HEAD_NORM_EPS = 1e-6
CONV_W = 3
N_EXPERTS = 16
N_GROUPS = 4
EXPERTS_PER_GROUP = N_EXPERTS // N_GROUPS
D_EXPERT = 256
LN_EPS = 1e-5
ALPHA = (2.0 * DEPTH) ** 0.25

LANES = 128
BF16_SUBLANES = 16
VMEM_LIMIT = 56 * 1024 * 1024

ROW_TILE = 512
MOE_ROW_TILE = 1024


def _params(*sem):
    return pltpu.CompilerParams(dimension_semantics=sem, vmem_limit_bytes=VMEM_LIMIT)


def _const_spec(shape):
    nd = len(shape)
    return pl.BlockSpec(shape, lambda *_: (0,) * nd)


def _dot(a, b):
    return jnp.dot(a, b, preferred_element_type=F32)


def _layer_norm(v, g, b):
    mu = jnp.mean(v, axis=-1, keepdims=True)
    d = v - mu
    var = jnp.mean(d * d, axis=-1, keepdims=True)
    return d * lax.rsqrt(var + LN_EPS) * g + b


def _split3(v):
    a = v.astype(BF16)
    r = v - a.astype(F32)
    b = r.astype(BF16)
    c = (r - b.astype(F32)).astype(BF16)
    return a, b, c


def _mlstm_inproj_kernel(x_ref, w_ref, wg_ref, bg_ref, qkvo_ref, gc_ref, gr_ref):
    tm = x_ref.shape[0]
    xb = x_ref[...].astype(BF16)
    for j in range(4):
        sl = slice(j * D_MODEL, (j + 1) * D_MODEL)
        qkvo_ref[:, sl] = _dot(xb, w_ref[:, sl]).astype(BF16)

    gates = _dot(xb, wg_ref[...]) + bg_ref[...]
    logf = jax.nn.log_sigmoid(gates)
    lane = lax.broadcasted_iota(jnp.int32, (CHUNK, LANES), 1)
    is_f_fw = (lane >= N_HEADS) & (lane < 2 * N_HEADS)
    is_f_bw = (lane >= 3 * N_HEADS) & (lane < 4 * N_HEADS)
    ri = lax.broadcasted_iota(jnp.int32, (CHUNK, CHUNK), 0)
    ci = lax.broadcasted_iota(jnp.int32, (CHUNK, CHUNK), 1)
    tri_lo = jnp.where(ci <= ri, 1.0, 0.0).astype(BF16)
    tri_up = jnp.where(ci >= ri, 1.0, 0.0).astype(BF16)
    for c in range(tm // CHUNK):
        rows = slice(c * CHUNK, (c + 1) * CHUNK)
        a, b, d = _split3(logf[rows])
        pre = _dot(tri_lo, a) + _dot(tri_lo, b) + _dot(tri_lo, d)
        suf = _dot(tri_up, a) + _dot(tri_up, b) + _dot(tri_up, d)
        g = jnp.where(is_f_fw, pre, jnp.where(is_f_bw, suf, gates[rows]))
        gc_ref[rows, :] = g
        gr_ref[:, rows] = g.T[:N_GATE_COLS, :]


def _mlstm_inproj(x, w, wg, bg):
    t = x.shape[0]
    tm = min(ROW_TILE, t)
    return pl.pallas_call(
        _mlstm_inproj_kernel,
        grid=(t // tm,),
        in_specs=[
            pl.BlockSpec((tm, D_MODEL), lambda i: (i, 0)),
            _const_spec(w.shape),
            _const_spec(wg.shape),
            _const_spec(bg.shape),
        ],
        out_specs=[
            pl.BlockSpec((tm, 4 * D_MODEL), lambda i: (i, 0)),
            pl.BlockSpec((tm, LANES), lambda i: (i, 0)),
            pl.BlockSpec((N_GATE_COLS, tm), lambda i: (0, i)),
        ],
        out_shape=[
            jax.ShapeDtypeStruct((t, 4 * D_MODEL), BF16),
            jax.ShapeDtypeStruct((t, LANES), F32),
            jax.ShapeDtypeStruct((N_GATE_COLS, t), F32),
        ],
        compiler_params=_params("arbitrary"),
        name="mlstm_inproj",
    )(x, w, wg, bg)


def _mlstm_chunk(q, k, v, f_col, a_col, a_row, f_tot, causal, c_ref, n_row, m_prev):
    d_log = jnp.where(causal, f_col + a_row, -jnp.inf)
    inter = f_col + m_prev
    m_row = jnp.maximum(jnp.max(d_log, axis=1, keepdims=True), inter)
    w = jnp.exp(d_log - m_row)
    s = lax.dot_general(q, k, (((1,), (1,)), ((), ())), preferred_element_type=F32) * w
    a_inter = jnp.exp(inter - m_row)
    c_prev = c_ref[...]
    num = _dot(s.astype(BF16), v) + a_inter * _dot(q, c_prev.astype(BF16))
    qn = jnp.sum(q.astype(F32) * n_row, axis=1, keepdims=True)
    den = jnp.sum(s, axis=1, keepdims=True) + a_inter * qn
    h = num / jnp.maximum(jnp.abs(den), jnp.exp(-m_row))
    m_new = jnp.maximum(f_tot + m_prev, jnp.max(f_tot + a_row, axis=1, keepdims=True))
    wk = jnp.exp(f_tot + a_col - m_new)
    decay = jnp.exp(f_tot + m_prev - m_new)
    kw = k.astype(F32) * wk
    c_ref[...] = decay * c_prev + _dot(kw.T.astype(BF16), v)
    n_new = decay * n_row + jnp.sum(kw, axis=0, keepdims=True)
    return h, n_new, m_new


def _mlstm_core_kernel(q_ref, k_ref, v_ref, o_ref, gc_ref, gr_ref, ng_ref, y_ref, h_scr, c_scr):
    seq = q_ref.shape[0]
    nc = seq // CHUNK
    head = pl.program_id(1)
    lane = lax.broadcasted_iota(jnp.int32, (CHUNK, LANES), 1)
    sub = lax.broadcasted_iota(jnp.int32, (N_GATE_COLS, CHUNK), 0)
    ri = lax.broadcasted_iota(jnp.int32, (CHUNK, CHUNK), 0)
    ci = lax.broadcasted_iota(jnp.int32, (CHUNK, CHUNK), 1)

    def gate_col(gcc, idx):
        return jnp.sum(jnp.where(lane == idx, gcc, 0.0), axis=1, keepdims=True)

    def gate_row(grc, idx):
        return jnp.sum(jnp.where(sub == idx, grc, 0.0), axis=0, keepdims=True)

    def run(direction, finalize):
        base = 2 * N_HEADS * direction + head
        causal = (ci <= ri) if direction == 0 else (ci >= ri)
        c_scr[...] = jnp.zeros_like(c_scr)

        def body(t, carry):
            n_row, m_prev = carry
            c = t if direction == 0 else nc - 1 - t
            r0 = pl.multiple_of(c * CHUNK, CHUNK)
            rows = pl.ds(r0, CHUNK)
            gcc = gc_ref[rows, :]
            grc = gr_ref[:, rows]
            i_col, f_col = gate_col(gcc, base), gate_col(gcc, base + N_HEADS)
            i_row, f_row = gate_row(grc, base), gate_row(grc, base + N_HEADS)
            f_tot = f_row[:, CHUNK - 1:CHUNK] if direction == 0 else f_row[:, 0:1]
            h, n_new, m_new = _mlstm_chunk(
                q_ref[rows, :], k_ref[rows, :], v_ref[rows, :],
                f_col, i_col - f_col, i_row - f_row, f_tot, causal, c_scr, n_row, m_prev)
            if not finalize:
                h_scr[rows, :] = h
            else:
                hh = h_scr[rows, :] + h
                mu = jnp.mean(hh, axis=1, keepdims=True)
                d = hh - mu
                var = jnp.mean(d * d, axis=1, keepdims=True)
                hn = d * lax.rsqrt(var + HEAD_NORM_EPS) * ng_ref[...]
                y_ref[rows, :] = (jax.nn.sigmoid(o_ref[rows, :].astype(F32)) * hn).astype(BF16)
            return n_new, m_new

        init = (jnp.zeros((1, HEAD_DIM), F32), jnp.full((1, 1), M_INIT, F32))
        lax.fori_loop(0, nc, body, init)

    run(0, False)
    run(1, True)


def _mlstm_core(qkvo, gc, gr, norm_g, bsz, seq):
    blk = lambda off: pl.BlockSpec((seq, HEAD_DIM), lambda b, h: (b, off + h))
    return pl.pallas_call(
        _mlstm_core_kernel,
        grid=(bsz, N_HEADS),
        in_specs=[
            blk(0), blk(N_HEADS), blk(2 * N_HEADS), blk(3 * N_HEADS),
            pl.BlockSpec((seq, LANES), lambda b, h: (b, 0)),
            pl.BlockSpec((N_GATE_COLS, seq), lambda b, h: (0, b)),
            pl.BlockSpec((1, HEAD_DIM), lambda b, h: (0, h)),
        ],
        out_specs=pl.BlockSpec((seq, HEAD_DIM), lambda b, h: (b, h)),
        out_shape=jax.ShapeDtypeStruct((bsz * seq, D_MODEL), BF16),
        scratch_shapes=[pltpu.VMEM((seq, HEAD_DIM), F32), pltpu.VMEM((HEAD_DIM, HEAD_DIM), F32)],
        compiler_params=_params("arbitrary", "arbitrary"),
        name="mlstm_core",
    )(qkvo, qkvo, qkvo, qkvo, gc, gr, norm_g)


def _router_combine(x1, rw2_ref, rb_ref):
    tm = x1.shape[0]
    hi = x1.astype(BF16)
    lo = (x1 - hi.astype(F32)).astype(BF16)
    ph = _dot(hi, rw2_ref[...])
    pq = _dot(lo, rw2_ref[...])
    logits = ph[:, :LANES] + ph[:, LANES:] + pq[:, :LANES] + rb_ref[...]
    lt = logits.T[:N_EXPERTS, :]
    e = jnp.exp(lt - jnp.max(lt, axis=0, keepdims=True))
    p = e / jnp.sum(e, axis=0, keepdims=True)
    best = None
    for g in range(N_GROUPS):
        a, b, c, d = (p[EXPERTS_PER_GROUP * g + i:EXPERTS_PER_GROUP * g + i + 1, :]
                      for i in range(EXPERTS_PER_GROUP))
        hi1, lo1 = jnp.maximum(a, b), jnp.minimum(a, b)
        hi2, lo2 = jnp.maximum(c, d), jnp.minimum(c, d)
        score = jnp.maximum(hi1, hi2) + jnp.maximum(jnp.minimum(hi1, hi2), jnp.maximum(lo1, lo2))
        if best is None:
            best, g_sel = score, jnp.zeros((1, tm), jnp.int32)
        else:
            upd = score > best
            best = jnp.where(upd, score, best)
            g_sel = jnp.where(upd, g, g_sel)
    eid = lax.broadcasted_iota(jnp.int32, (N_EXPERTS, tm), 0)
    pm = jnp.where(eid // EXPERTS_PER_GROUP == g_sel, p, -1.0)
    v1 = jnp.max(pm, axis=0, keepdims=True)
    i1 = jnp.min(jnp.where(pm == v1, eid, N_EXPERTS), axis=0, keepdims=True)
    pm2 = jnp.where(eid == i1, -2.0, pm)
    v2 = jnp.max(pm2, axis=0, keepdims=True)
    i2 = jnp.min(jnp.where(pm2 == v2, eid, N_EXPERTS), axis=0, keepdims=True)
    tot = v1 + v2
    comb = jnp.where(eid == i1, v1 / tot, jnp.where(eid == i2, v2 / tot, 0.0))
    comb = jnp.concatenate([comb, jnp.zeros((LANES - N_EXPERTS, tm), F32)], axis=0)
    return comb.T


def _post_mixer_tail(z, x, wo_ref, g_ref, b_ref, rw2_ref, rb_ref, x1_ref, comb_ref):
    mix = _dot(z, wo_ref[...])
    x1 = _layer_norm(ALPHA * x + mix, g_ref[...], b_ref[...])
    x1_ref[...] = x1
    comb_ref[...] = _router_combine(x1, rw2_ref, rb_ref)


def _mlstm_post_kernel(y_ref, x_ref, wo_ref, g_ref, b_ref, rw2_ref, rb_ref, x1_ref, comb_ref):
    _post_mixer_tail(y_ref[...], x_ref[...], wo_ref, g_ref, b_ref, rw2_ref, rb_ref, x1_ref, comb_ref)


def _mlstm_post(y, x, wo, g, b, rw2, rb):
    t = x.shape[0]
    tm = min(ROW_TILE, t)
    row = lambda w: pl.BlockSpec((tm, w), lambda i: (i, 0))
    return pl.pallas_call(
        _mlstm_post_kernel,
        grid=(t // tm,),
        in_specs=[row(D_MODEL), row(D_MODEL), _const_spec(wo.shape), _const_spec(g.shape),
                  _const_spec(b.shape), _const_spec(rw2.shape), _const_spec(rb.shape)],
        out_specs=[row(D_MODEL), row(LANES)],
        out_shape=[jax.ShapeDtypeStruct((t, D_MODEL), F32), jax.ShapeDtypeStruct((t, LANES), F32)],
        compiler_params=_params("arbitrary"),
        name="mlstm_post",
    )(y, x, wo, g, b, rw2, rb)


def _moe_kernel(x_ref, comb_ref, wgu_ref, wd_ref, g_ref, b_ref, out_ref, xb_scr, acc_scr):
    e = pl.program_id(1)

    @pl.when(e == 0)
    def _():
        xb_scr[...] = x_ref[...].astype(BF16)
        acc_scr[...] = jnp.zeros_like(acc_scr)

    gu = _dot(xb_scr[...], wgu_ref[...])
    hid = jax.nn.silu(gu[:, :D_EXPERT]) * gu[:, D_EXPERT:]
    lane = lax.broadcasted_iota(jnp.int32, comb_ref.shape, 1)
    c_col = jnp.sum(jnp.where(lane == e, comb_ref[...], 0.0), axis=1, keepdims=True)
    acc_scr[...] += _dot((hid * c_col).astype(BF16), wd_ref[...])

    @pl.when(e == N_EXPERTS - 1)
    def _():
        out_ref[...] = _layer_norm(ALPHA * x_ref[...] + acc_scr[...], g_ref[...], b_ref[...])


def _moe(x1, comb, wgu, wd, g, b):
    t = x1.shape[0]
    tm = min(MOE_ROW_TILE, t)
    return pl.pallas_call(
        _moe_kernel,
        grid=(t // tm, N_EXPERTS),
        in_specs=[
            pl.BlockSpec((tm, D_MODEL), lambda i, e: (i, 0)),
            pl.BlockSpec((tm, LANES), lambda i, e: (i, 0)),
            pl.BlockSpec((None, D_MODEL, 2 * D_EXPERT), lambda i, e: (e, 0, 0)),
            pl.BlockSpec((None, D_EXPERT, D_MODEL), lambda i, e: (e, 0, 0)),
            _const_spec(g.shape), _const_spec(b.shape),
        ],
        out_specs=pl.BlockSpec((tm, D_MODEL), lambda i, e: (i, 0)),
        out_shape=jax.ShapeDtypeStruct((t, D_MODEL), F32),
        scratch_shapes=[pltpu.VMEM((tm, D_MODEL), BF16), pltpu.VMEM((tm, D_MODEL), F32)],
        compiler_params=_params("arbitrary", "arbitrary"),
        name="moe",
    )(x1, comb, wgu, wd, g, b)


def _conv_inproj_kernel(x_ref, w_ref, bg_ref, u_ref):
    xb = x_ref[...].astype(BF16)
    bg_ref[...] = _dot(xb, w_ref[:, :D_MODEL]).astype(BF16)
    c_gate = _dot(xb, w_ref[:, D_MODEL:2 * D_MODEL])
    hv = _dot(xb, w_ref[:, 2 * D_MODEL:])
    u_ref[...] = (c_gate * hv).astype(BF16)


def _conv_inproj(x, w):
    t = x.shape[0]
    tm = min(ROW_TILE, t)
    row = pl.BlockSpec((tm, D_MODEL), lambda i: (i, 0))
    return pl.pallas_call(
        _conv_inproj_kernel,
        grid=(t // tm,),
        in_specs=[row, _const_spec(w.shape)],
        out_specs=[row, row],
        out_shape=[jax.ShapeDtypeStruct((t, D_MODEL), BF16)] * 2,
        compiler_params=_params("arbitrary"),
        name="conv_inproj",
    )(x, w)


def _conv_post_kernel(tiles_per_seq, u_ref, up_ref, un_ref, bg_ref, x_ref, cw_ref, wo_ref,
                      g_ref, b_ref, rw2_ref, rb_ref, x1_ref, comb_ref):
    tm = u_ref.shape[0]
    pos = pl.program_id(0) % tiles_per_seq
    u = u_ref[...].astype(F32)
    prev_row = jnp.where(pos == 0, 0.0, up_ref[BF16_SUBLANES - 1:BF16_SUBLANES, :].astype(F32))
    next_row = jnp.where(pos == tiles_per_seq - 1, 0.0, un_ref[0:1, :].astype(F32))
    ri = lax.broadcasted_iota(jnp.int32, (tm, 1), 0)
    u_m1 = jnp.where(ri == 0, prev_row, pltpu.roll(u, 1, axis=0))
    u_p1 = jnp.where(ri == tm - 1, next_row, pltpu.roll(u, tm - 1, axis=0))
    cw = cw_ref[...]
    conv = cw[0:1, :] * u_m1 + cw[1:2, :] * u + cw[2:3, :] * u_p1
    z = (bg_ref[...].astype(F32) * conv).astype(BF16)
    _post_mixer_tail(z, x_ref[...], wo_ref, g_ref, b_ref, rw2_ref, rb_ref, x1_ref, comb_ref)


def _conv_post(u, bg, x, cw, wo, g, b, rw2, rb, seq):
    t = x.shape[0]
    tm = min(ROW_TILE, seq)
    halo_per_tile = tm // BF16_SUBLANES
    n_halo = t // BF16_SUBLANES
    row = lambda w: pl.BlockSpec((tm, w), lambda i: (i, 0))
    prev_spec = pl.BlockSpec((BF16_SUBLANES, D_MODEL),
                             lambda i: (jnp.maximum(i * halo_per_tile - 1, 0), 0))
    next_spec = pl.BlockSpec((BF16_SUBLANES, D_MODEL),
                             lambda i: (jnp.minimum((i + 1) * halo_per_tile, n_halo - 1), 0))
    return pl.pallas_call(
        functools.partial(_conv_post_kernel, seq // tm),
        grid=(t // tm,),
        in_specs=[row(D_MODEL), prev_spec, next_spec, row(D_MODEL), row(D_MODEL),
                  _const_spec(cw.shape), _const_spec(wo.shape), _const_spec(g.shape),
                  _const_spec(b.shape), _const_spec(rw2.shape), _const_spec(rb.shape)],
        out_specs=[row(D_MODEL), row(LANES)],
        out_shape=[jax.ShapeDtypeStruct((t, D_MODEL), F32), jax.ShapeDtypeStruct((t, LANES), F32)],
        compiler_params=_params("arbitrary"),
        name="conv_post",
    )(u, u, u, bg, x, cw, wo, g, b, rw2, rb)


def _prep_weights(router_w, router_b, mlstm_w_in, mlstm_b_gate, mlstm_norm_g, mlstm_w_out,
                  conv_w_in, conv_w, conv_w_out, ln1_g, ln1_b, moe_w_gu, moe_w_down, ln2_g, ln2_b):
    row = lambda v: v.reshape(1, -1).astype(F32)
    w_in = mlstm_w_in[0]
    col_scale = jnp.concatenate([jnp.ones((D_MODEL,), F32), jnp.full((D_MODEL,), HEAD_DIM ** -0.5, F32),
                                 jnp.ones((2 * D_MODEL,), F32)])
    w_qkvo = (w_in[:, :4 * D_MODEL] * col_scale).astype(BF16)
    pad = LANES - N_GATE_COLS
    w_gate = jnp.pad(w_in[:, 4 * D_MODEL:], ((0, 0), (0, pad))).astype(BF16)
    b_gate = jnp.pad(mlstm_b_gate[0], (0, pad)).reshape(1, LANES).astype(F32)
    rw_hi = router_w.astype(BF16)
    rw_lo = (router_w - rw_hi.astype(F32)).astype(BF16)
    padr = LANES - N_EXPERTS
    rw2 = jnp.concatenate([jnp.pad(rw_hi, ((0, 0), (0, padr))), jnp.pad(rw_lo, ((0, 0), (0, padr)))], axis=1)
    rb = jnp.pad(router_b.astype(F32), (0, padr)).reshape(1, LANES)
    return dict(
        w_qkvo=w_qkvo, w_gate=w_gate, b_gate=b_gate, norm_g=row(mlstm_norm_g[0]),
        mlstm_w_out=mlstm_w_out[0].astype(BF16),
        conv_w_in=conv_w_in[0].astype(BF16), conv_w=conv_w[0].astype(F32),
        conv_w_out=conv_w_out[0].astype(BF16),
        rw2=rw2, rb=rb,
        ln1_g=[row(ln1_g[i]) for i in range(DEPTH)], ln1_b=[row(ln1_b[i]) for i in range(DEPTH)],
        ln2_g=[row(ln2_g[i]) for i in range(DEPTH)], ln2_b=[row(ln2_b[i]) for i in range(DEPTH)],
        wgu=[moe_w_gu[i].astype(BF16) for i in range(DEPTH)],
        wd=[moe_w_down[i].astype(BF16) for i in range(DEPTH)],
    )


def _trunk(x3, p):
    bsz, seq, _ = x3.shape
    x = x3.reshape(bsz * seq, D_MODEL)
    qkvo, gc, gr = _mlstm_inproj(x, p["w_qkvo"], p["w_gate"], p["b_gate"])
    y = _mlstm_core(qkvo, gc, gr, p["norm_g"], bsz, seq)
    x1, comb = _mlstm_post(y, x, p["mlstm_w_out"], p["ln1_g"][0], p["ln1_b"][0], p["rw2"], p["rb"])
    x = _moe(x1, comb, p["wgu"][0], p["wd"][0], p["ln2_g"][0], p["ln2_b"][0])
    bg, u = _conv_inproj(x, p["conv_w_in"])
    x1, comb = _conv_post(u, bg, x, p["conv_w"], p["conv_w_out"], p["ln1_g"][1], p["ln1_b"][1],
                          p["rw2"], p["rb"], seq)
    x = _moe(x1, comb, p["wgu"][1], p["wd"][1], p["ln2_g"][1], p["ln2_b"][1])
    return x.reshape(bsz, seq, D_MODEL)


def kernel(x_prompt, x_sample, router_w, router_b, mlstm_w_in, mlstm_b_gate, mlstm_norm_g, mlstm_w_out,
           conv_w_in, conv_w, conv_w_out, ln1_g, ln1_b, moe_w_gu, moe_w_down, ln2_g, ln2_b):
    p = _prep_weights(router_w, router_b, mlstm_w_in, mlstm_b_gate, mlstm_norm_g, mlstm_w_out,
                      conv_w_in, conv_w, conv_w_out, ln1_g, ln1_b, moe_w_gu, moe_w_down, ln2_g, ln2_b)
    return (_trunk(x_prompt, p), _trunk(x_sample, p))
```

```python
import functools

import jax
import jax.numpy as jnp
import numpy as np
from jax import lax
from jax.experimental import pallas as pl
from jax.experimental.pallas import tpu as pltpu

F32 = jnp.float32
BF16 = jnp.bfloat16
I32 = jnp.int32

D_MODEL = 1024
DEPTH = 2
N_HEADS = 4
HEAD_DIM = D_MODEL // N_HEADS
CHUNK = 128
N_GATE_COLS = 4 * N_HEADS
M_INIT = -1e30
HEAD_NORM_EPS = 1e-6
CONV_W = 3
N_EXPERTS = 16
N_GROUPS = 4
EXPERTS_PER_GROUP = N_EXPERTS // N_GROUPS
D_EXPERT = 256
LN_EPS = 1e-5
ALPHA = (2.0 * DEPTH) ** 0.25

LANES = 128
SUBLANES = 8
BF16_SUBLANES = 16
VMEM_LIMIT = 56 * 1024 * 1024

ROW_TILE = 512
SORT_TILE = 256

_PAIRS = [(a, b) for a in range(EXPERTS_PER_GROUP) for b in range(a + 1, EXPERTS_PER_GROUP)]
N_CLASSES = N_GROUPS * len(_PAIRS)
CLASS_ROWS = 32
_CLASS_LO = np.array([EXPERTS_PER_GROUP * g + a for g in range(N_GROUPS) for a, _ in _PAIRS], np.int32)
_CLASS_HI = np.array([EXPERTS_PER_GROUP * g + b for g in range(N_GROUPS) for _, b in _PAIRS], np.int32)
AUG = D_MODEL + LANES


def _params(*sem):
    return pltpu.CompilerParams(dimension_semantics=sem, vmem_limit_bytes=VMEM_LIMIT)


def _const_spec(shape):
    nd = len(shape)
    return pl.BlockSpec(shape, lambda *_: (0,) * nd)


def _dot(a, b):
    return jnp.dot(a, b, preferred_element_type=F32)


def _layer_norm(v, g, b):
    mu = jnp.mean(v, axis=-1, keepdims=True)
    d = v - mu
    var = jnp.mean(d * d, axis=-1, keepdims=True)
    return d * lax.rsqrt(var + LN_EPS) * g + b


def _split3(v):
    a = v.astype(BF16)
    r = v - a.astype(F32)
    b = r.astype(BF16)
    c = (r - b.astype(F32)).astype(BF16)
    return a, b, c


def _mlstm_inproj_kernel(x_ref, w_ref, wg_ref, bg_ref, qkvo_ref, gc_ref, gr_ref):
    tm = x_ref.shape[0]
    xb = x_ref[...].astype(BF16)
    for j in range(4):
        sl = slice(j * D_MODEL, (j + 1) * D_MODEL)
        qkvo_ref[:, sl] = _dot(xb, w_ref[:, sl]).astype(BF16)

    gates = _dot(xb, wg_ref[...]) + bg_ref[...]
    logf = jax.nn.log_sigmoid(gates)
    lane = lax.broadcasted_iota(I32, (CHUNK, LANES), 1)
    is_f_fw = (lane >= N_HEADS) & (lane < 2 * N_HEADS)
    is_f_bw = (lane >= 3 * N_HEADS) & (lane < 4 * N_HEADS)
    ri = lax.broadcasted_iota(I32, (CHUNK, CHUNK), 0)
    ci = lax.broadcasted_iota(I32, (CHUNK, CHUNK), 1)
    tri_lo = jnp.where(ci <= ri, 1.0, 0.0).astype(BF16)
    tri_up = jnp.where(ci >= ri, 1.0, 0.0).astype(BF16)
    for c in range(tm // CHUNK):
        rows = slice(c * CHUNK, (c + 1) * CHUNK)
        a, b, d = _split3(logf[rows])
        pre = _dot(tri_lo, a) + _dot(tri_lo, b) + _dot(tri_lo, d)
        suf = _dot(tri_up, a) + _dot(tri_up, b) + _dot(tri_up, d)
        g = jnp.where(is_f_fw, pre, jnp.where(is_f_bw, suf, gates[rows]))
        gc_ref[rows, :] = g
        gr_ref[:, rows] = g.T[:N_GATE_COLS, :]


def _mlstm_inproj(x, w, wg, bg):
    t = x.shape[0]
    tm = min(ROW_TILE, t)
    return pl.pallas_call(
        _mlstm_inproj_kernel,
        grid=(t // tm,),
        in_specs=[
            pl.BlockSpec((tm, D_MODEL), lambda i: (i, 0)),
            _const_spec(w.shape),
            _const_spec(wg.shape),
            _const_spec(bg.shape),
        ],
        out_specs=[
            pl.BlockSpec((tm, 4 * D_MODEL), lambda i: (i, 0)),
            pl.BlockSpec((tm, LANES), lambda i: (i, 0)),
            pl.BlockSpec((N_GATE_COLS, tm), lambda i: (0, i)),
        ],
        out_shape=[
            jax.ShapeDtypeStruct((t, 4 * D_MODEL), BF16),
            jax.ShapeDtypeStruct((t, LANES), F32),
            jax.ShapeDtypeStruct((N_GATE_COLS, t), F32),
        ],
        compiler_params=_params("arbitrary"),
        name="mlstm_inproj",
    )(x, w, wg, bg)


def _mlstm_chunk(q, k, v, f_col, a_col, a_row, f_tot, causal, c_ref, n_row, m_prev):
    d_log = jnp.where(causal, f_col + a_row, -jnp.inf)
    inter = f_col + m_prev
    m_row = jnp.maximum(jnp.max(d_log, axis=1, keepdims=True), inter)
    w = jnp.exp(d_log - m_row)
    s = lax.dot_general(q, k, (((1,), (1,)), ((), ())), preferred_element_type=F32) * w
    a_inter = jnp.exp(inter - m_row)
    c_prev = c_ref[...]
    num = _dot(s.astype(BF16), v) + a_inter * _dot(q, c_prev.astype(BF16))
    qn = jnp.sum(q.astype(F32) * n_row, axis=1, keepdims=True)
    den = jnp.sum(s, axis=1, keepdims=True) + a_inter * qn
    h = num / jnp.maximum(jnp.abs(den), jnp.exp(-m_row))
    m_new = jnp.maximum(f_tot + m_prev, jnp.max(f_tot + a_row, axis=1, keepdims=True))
    wk = jnp.exp(f_tot + a_col - m_new)
    decay = jnp.exp(f_tot + m_prev - m_new)
    kw = k.astype(F32) * wk
    c_ref[...] = decay * c_prev + _dot(kw.T.astype(BF16), v)
    n_new = decay * n_row + jnp.sum(kw, axis=0, keepdims=True)
    return h, n_new, m_new


def _mlstm_core_kernel(q_ref, k_ref, v_ref, o_ref, gc_ref, gr_ref, ng_ref, y_ref, h_scr, c_scr):
    seq = q_ref.shape[0]
    nc = seq // CHUNK
    head = pl.program_id(1)
    lane = lax.broadcasted_iota(I32, (CHUNK, LANES), 1)
    sub = lax.broadcasted_iota(I32, (N_GATE_COLS, CHUNK), 0)
    ri = lax.broadcasted_iota(I32, (CHUNK, CHUNK), 0)
    ci = lax.broadcasted_iota(I32, (CHUNK, CHUNK), 1)

    def gate_col(gcc, idx):
        return jnp.sum(jnp.where(lane == idx, gcc, 0.0), axis=1, keepdims=True)

    def gate_row(grc, idx):
        return jnp.sum(jnp.where(sub == idx, grc, 0.0), axis=0, keepdims=True)

    def run(direction, finalize):
        base = 2 * N_HEADS * direction + head
        causal = (ci <= ri) if direction == 0 else (ci >= ri)
        c_scr[...] = jnp.zeros_like(c_scr)

        def body(t, carry):
            n_row, m_prev = carry
            c = t if direction == 0 else nc - 1 - t
            r0 = pl.multiple_of(c * CHUNK, CHUNK)
            rows = pl.ds(r0, CHUNK)
            gcc = gc_ref[rows, :]
            grc = gr_ref[:, rows]
            i_col, f_col = gate_col(gcc, base), gate_col(gcc, base + N_HEADS)
            i_row, f_row = gate_row(grc, base), gate_row(grc, base + N_HEADS)
            f_tot = f_row[:, CHUNK - 1:CHUNK] if direction == 0 else f_row[:, 0:1]
            h, n_new, m_new = _mlstm_chunk(
                q_ref[rows, :], k_ref[rows, :], v_ref[rows, :],
                f_col, i_col - f_col, i_row - f_row, f_tot, causal, c_scr, n_row, m_prev)
            if not finalize:
                h_scr[rows, :] = h
            else:
                hh = h_scr[rows, :] + h
                mu = jnp.mean(hh, axis=1, keepdims=True)
                d = hh - mu
                var = jnp.mean(d * d, axis=1, keepdims=True)
                hn = d * lax.rsqrt(var + HEAD_NORM_EPS) * ng_ref[...]
                y_ref[rows, :] = (jax.nn.sigmoid(o_ref[rows, :].astype(F32)) * hn).astype(BF16)
            return n_new, m_new

        init = (jnp.zeros((1, HEAD_DIM), F32), jnp.full((1, 1), M_INIT, F32))
        lax.fori_loop(0, nc, body, init)

    run(0, False)
    run(1, True)


def _mlstm_core(qkvo, gc, gr, norm_g, bsz, seq):
    blk = lambda off: pl.BlockSpec((seq, HEAD_DIM), lambda b, h: (b, off + h))
    return pl.pallas_call(
        _mlstm_core_kernel,
        grid=(bsz, N_HEADS),
        in_specs=[
            blk(0), blk(N_HEADS), blk(2 * N_HEADS), blk(3 * N_HEADS),
            pl.BlockSpec((seq, LANES), lambda b, h: (b, 0)),
            pl.BlockSpec((N_GATE_COLS, seq), lambda b, h: (0, b)),
            pl.BlockSpec((1, HEAD_DIM), lambda b, h: (0, h)),
        ],
        out_specs=pl.BlockSpec((seq, HEAD_DIM), lambda b, h: (b, h)),
        out_shape=jax.ShapeDtypeStruct((bsz * seq, D_MODEL), BF16),
        scratch_shapes=[pltpu.VMEM((seq, HEAD_DIM), F32), pltpu.VMEM((HEAD_DIM, HEAD_DIM), F32)],
        compiler_params=_params("arbitrary", "arbitrary"),
        name="mlstm_core",
    )(qkvo, qkvo, qkvo, qkvo, gc, gr, norm_g)


def _route(x1, rw2_ref, rb_ref):
    tm = x1.shape[0]
    hi = x1.astype(BF16)
    lo = (x1 - hi.astype(F32)).astype(BF16)
    ph = _dot(hi, rw2_ref[...])
    pq = _dot(lo, rw2_ref[...])
    logits = ph[:, :LANES] + ph[:, LANES:] + pq[:, :LANES] + rb_ref[...]
    lt = logits.T[:N_EXPERTS, :]
    e = jnp.exp(lt - jnp.max(lt, axis=0, keepdims=True))
    p = e / jnp.sum(e, axis=0, keepdims=True)
    best = None
    for g in range(N_GROUPS):
        a, b, c, d = (p[EXPERTS_PER_GROUP * g + i:EXPERTS_PER_GROUP * g + i + 1, :]
                      for i in range(EXPERTS_PER_GROUP))
        hi1, lo1 = jnp.maximum(a, b), jnp.minimum(a, b)
        hi2, lo2 = jnp.maximum(c, d), jnp.minimum(c, d)
        score = jnp.maximum(hi1, hi2) + jnp.maximum(jnp.minimum(hi1, hi2), jnp.maximum(lo1, lo2))
        if best is None:
            best, g_sel = score, jnp.zeros((1, tm), I32)
        else:
            upd = score > best
            best = jnp.where(upd, score, best)
            g_sel = jnp.where(upd, g, g_sel)
    eid = lax.broadcasted_iota(I32, (N_EXPERTS, tm), 0)
    pm = jnp.where(eid // EXPERTS_PER_GROUP == g_sel, p, -1.0)
    v1 = jnp.max(pm, axis=0, keepdims=True)
    i1 = jnp.min(jnp.where(pm == v1, eid, N_EXPERTS), axis=0, keepdims=True)
    pm2 = jnp.where(eid == i1, -2.0, pm)
    v2 = jnp.max(pm2, axis=0, keepdims=True)
    i2 = jnp.min(jnp.where(pm2 == v2, eid, N_EXPERTS), axis=0, keepdims=True)
    tot = v1 + v2
    w1, w2 = v1 / tot, v2 / tot
    first_lo = i1 < i2
    e_lo, e_hi = jnp.where(first_lo, i1, i2), jnp.where(first_lo, i2, i1)
    c_lo, c_hi = jnp.where(first_lo, w1, w2), jnp.where(first_lo, w2, w1)
    a = e_lo - EXPERTS_PER_GROUP * g_sel
    b = e_hi - EXPERTS_PER_GROUP * g_sel
    pair = jnp.where(a == 0, 0, jnp.where(a == 1, 3, 5)) + (b - a - 1)
    cls = g_sel * len(_PAIRS) + pair

    cid = lax.broadcasted_iota(I32, (CLASS_ROWS, tm), 0)
    member = cid == cls
    onehot = jnp.where(member, 1.0, 0.0).astype(BF16)
    ri = lax.broadcasted_iota(I32, (tm, tm), 0)
    ci = lax.broadcasted_iota(I32, (tm, tm), 1)
    earlier = jnp.where(ri < ci, 1.0, 0.0).astype(BF16)
    prefix = _dot(onehot, earlier)
    rank = jnp.sum(jnp.where(member, prefix, 0.0), axis=0, keepdims=True).astype(I32)
    counts = _dot(onehot, jnp.ones((tm, LANES), BF16))

    sub = lax.broadcasted_iota(I32, (LANES, tm), 0)
    aux = jnp.where(sub == 0, c_lo, jnp.where(sub == 1, c_hi, 0.0)).T
    sub8 = lax.broadcasted_iota(I32, (SUBLANES, tm), 0)
    route = jnp.where(sub8 == 0, cls, jnp.where(sub8 == 1, rank, 0))
    return aux, route, counts


def _post_mixer_tail(z, x, wo_ref, g_ref, b_ref, rw2_ref, rb_ref, xa_ref, route_ref, cnt_ref):
    mix = _dot(z, wo_ref[...])
    x1 = _layer_norm(ALPHA * x + mix, g_ref[...], b_ref[...])
    aux, route, counts = _route(x1, rw2_ref, rb_ref)
    xa_ref[:, :D_MODEL] = x1
    xa_ref[:, D_MODEL:] = aux
    route_ref[...] = route
    cnt_ref[...] = counts


def _post_out(t, tm):
    specs = [pl.BlockSpec((tm, AUG), lambda i: (i, 0)),
             pl.BlockSpec((None, SUBLANES, tm), lambda i: (i, 0, 0)),
             pl.BlockSpec((None, CLASS_ROWS, LANES), lambda i: (i, 0, 0))]
    shapes = [jax.ShapeDtypeStruct((t, AUG), F32),
              jax.ShapeDtypeStruct((t // tm, SUBLANES, tm), I32),
              jax.ShapeDtypeStruct((t // tm, CLASS_ROWS, LANES), F32)]
    return specs, shapes


def _mlstm_post_kernel(y_ref, x_ref, wo_ref, g_ref, b_ref, rw2_ref, rb_ref, xa_ref, route_ref, cnt_ref):
    _post_mixer_tail(y_ref[...], x_ref[...], wo_ref, g_ref, b_ref, rw2_ref, rb_ref,
                     xa_ref, route_ref, cnt_ref)


def _mlstm_post(y, x, wo, g, b, rw2, rb):
    t = x.shape[0]
    tm = min(ROW_TILE, t)
    row = lambda w: pl.BlockSpec((tm, w), lambda i: (i, 0))
    out_specs, out_shape = _post_out(t, tm)
    return pl.pallas_call(
        _mlstm_post_kernel,
        grid=(t // tm,),
        in_specs=[row(D_MODEL), row(D_MODEL), _const_spec(wo.shape), _const_spec(g.shape),
                  _const_spec(b.shape), _const_spec(rw2.shape), _const_spec(rb.shape)],
        out_specs=out_specs,
        out_shape=out_shape,
        compiler_params=_params("arbitrary"),
        name="mlstm_post",
    )(y, x, wo, g, b, rw2, rb)


def _moe_plan(route, counts, tm):
    cls, rank = route[:, 0, :], route[:, 1, :]
    cnt = counts[:, :, 0].astype(I32)
    n_tiles = cls.shape[0]
    t = n_tiles * tm
    n_sorted = (t + N_CLASSES * SORT_TILE) // SORT_TILE
    tile_prefix = jnp.cumsum(cnt, axis=0) - cnt
    totals = jnp.sum(cnt, axis=0)
    padded = (totals + SORT_TILE - 1) // SORT_TILE * SORT_TILE
    class_end = jnp.cumsum(padded)
    class_base = class_end - padded
    off = class_base[None, :] + tile_prefix
    onehot = cls[:, :, None] == jnp.arange(CLASS_ROWS, dtype=I32)
    slot = rank + jnp.sum(jnp.where(onehot, off[:, None, :], 0), axis=-1)
    starts = jnp.arange(n_sorted, dtype=I32) * SORT_TILE
    tile_cls = jnp.minimum(jnp.searchsorted(class_end, starts, side="right"), N_CLASSES - 1).astype(I32)
    n_valid = jnp.clip(class_base[tile_cls] + totals[tile_cls] - starts, 0, SORT_TILE).astype(I32)
    tile_lo = jnp.asarray(_CLASS_LO)[tile_cls]
    tile_hi = jnp.asarray(_CLASS_HI)[tile_cls]
    return slot.reshape(n_tiles, 1, tm).astype(I32), tile_lo, tile_hi, n_valid, n_sorted


def _row_copy(src_ref, src_row, dst_ref, dst_row, sem):
    return pltpu.make_async_copy(src_ref.at[pl.ds(src_row, 1), :], dst_ref.at[pl.ds(dst_row, 1), :], sem)


def _moe_dispatch_kernel(slot_ref, x_ref, sorted_in_ref, sorted_ref, sem):
    del sorted_in_ref
    tm = x_ref.shape[0]

    def body(r, carry):
        _row_copy(x_ref, r, sorted_ref, slot_ref[0, r], sem).start()
        return carry

    lax.fori_loop(0, tm, body, 0, unroll=8)
    pltpu.make_async_copy(x_ref, sorted_ref.at[pl.ds(0, tm), :], sem).wait()


def _moe_dispatch(xa, slot, n_rows):
    t = xa.shape[0]
    tm = slot.shape[-1]
    sorted0 = jnp.zeros((n_rows, AUG), F32)
    return pl.pallas_call(
        _moe_dispatch_kernel,
        grid=(t // tm,),
        in_specs=[pl.BlockSpec((None, 1, tm), lambda i: (i, 0, 0), memory_space=pltpu.SMEM),
                  pl.BlockSpec((tm, AUG), lambda i: (i, 0)),
                  pl.BlockSpec(memory_space=pl.ANY)],
        out_specs=pl.BlockSpec(memory_space=pl.ANY),
        out_shape=jax.ShapeDtypeStruct((n_rows, AUG), F32),
        scratch_shapes=[pltpu.SemaphoreType.DMA],
        input_output_aliases={2: 0},
        compiler_params=_params("arbitrary"),
        name="moe_dispatch",
    )(slot, xa, sorted0)


def _moe_experts_kernel(lo_ref, hi_ref, nv_ref, xs_ref, wgl_ref, wgh_ref, wdl_ref, wdh_ref,
                        g_ref, b_ref, out_ref):
    del lo_ref, hi_ref
    occupied = nv_ref[pl.program_id(0)] > 0

    @pl.when(jnp.logical_not(occupied))
    def _():
        out_ref[...] = jnp.zeros_like(out_ref)

    @pl.when(occupied)
    def _():
        x = xs_ref[:, :D_MODEL]
        xb = x.astype(BF16)

        def hidden(wg_ref, c_col):
            gu = _dot(xb, wg_ref[...])
            return (jax.nn.silu(gu[:, :D_EXPERT]) * gu[:, D_EXPERT:] * c_col).astype(BF16)

        ffn = (_dot(hidden(wgl_ref, xs_ref[:, D_MODEL:D_MODEL + 1]), wdl_ref[...])
               + _dot(hidden(wgh_ref, xs_ref[:, D_MODEL + 1:D_MODEL + 2]), wdh_ref[...]))
        out_ref[...] = _layer_norm(ALPHA * x + ffn, g_ref[...], b_ref[...])


def _moe_experts(xs, tile_lo, tile_hi, n_valid, wgu, wd, g, b):
    n_rows = xs.shape[0]
    grid_spec = pltpu.PrefetchScalarGridSpec(
        num_scalar_prefetch=3,
        grid=(n_rows // SORT_TILE,),
        in_specs=[
            pl.BlockSpec((SORT_TILE, AUG), lambda j, lo, hi, nv: (j, 0)),
            pl.BlockSpec((None, D_MODEL, 2 * D_EXPERT), lambda j, lo, hi, nv: (lo[j], 0, 0)),
            pl.BlockSpec((None, D_MODEL, 2 * D_EXPERT), lambda j, lo, hi, nv: (hi[j], 0, 0)),
            pl.BlockSpec((None, D_EXPERT, D_MODEL), lambda j, lo, hi, nv: (lo[j], 0, 0)),
            pl.BlockSpec((None, D_EXPERT, D_MODEL), lambda j, lo, hi, nv: (hi[j], 0, 0)),
            pl.BlockSpec(g.shape, lambda j, lo, hi, nv: (0, 0)),
            pl.BlockSpec(b.shape, lambda j, lo, hi, nv: (0, 0)),
        ],
        out_specs=pl.BlockSpec((SORT_TILE, D_MODEL), lambda j, lo, hi, nv: (j, 0)),
    )
    return pl.pallas_call(
        _moe_experts_kernel,
        grid_spec=grid_spec,
        out_shape=jax.ShapeDtypeStruct((n_rows, D_MODEL), F32),
        compiler_params=_params("arbitrary"),
        name="moe_experts",
    )(tile_lo, tile_hi, n_valid, xs, wgu, wgu, wd, wd, g, b)


def _moe_return_kernel(slot_ref, sorted_ref, out_ref, sem):
    tm = out_ref.shape[0]

    def body(r, carry):
        _row_copy(sorted_ref, slot_ref[0, r], out_ref, r, sem).start()
        return carry

    lax.fori_loop(0, tm, body, 0, unroll=8)
    pltpu.make_async_copy(sorted_ref.at[pl.ds(0, tm), :], out_ref, sem).wait()


def _moe_return(ys, slot):
    n_tiles, _, tm = slot.shape
    t = n_tiles * tm
    return pl.pallas_call(
        _moe_return_kernel,
        grid=(n_tiles,),
        in_specs=[pl.BlockSpec((None, 1, tm), lambda i: (i, 0, 0), memory_space=pltpu.SMEM),
                  pl.BlockSpec(memory_space=pl.ANY)],
        out_specs=pl.BlockSpec((tm, D_MODEL), lambda i: (i, 0)),
        out_shape=jax.ShapeDtypeStruct((t, D_MODEL), F32),
        scratch_shapes=[pltpu.SemaphoreType.DMA],
        compiler_params=_params("arbitrary"),
        name="moe_return",
    )(slot, ys)


def _moe(xa, route, counts, wgu, wd, g, b):
    tm = route.shape[-1]
    slot, tile_lo, tile_hi, n_valid, n_sorted = _moe_plan(route, counts, tm)
    xs = _moe_dispatch(xa, slot, n_sorted * SORT_TILE)
    ys = _moe_experts(xs, tile_lo, tile_hi, n_valid, wgu, wd, g, b)
    return _moe_return(ys, slot)


def _conv_inproj_kernel(x_ref, w_ref, bg_ref, u_ref):
    xb = x_ref[...].astype(BF16)
    bg_ref[...] = _dot(xb, w_ref[:, :D_MODEL]).astype(BF16)
    c_gate = _dot(xb, w_ref[:, D_MODEL:2 * D_MODEL])
    hv = _dot(xb, w_ref[:, 2 * D_MODEL:])
    u_ref[...] = (c_gate * hv).astype(BF16)


def _conv_inproj(x, w):
    t = x.shape[0]
    tm = min(ROW_TILE, t)
    row = pl.BlockSpec((tm, D_MODEL), lambda i: (i, 0))
    return pl.pallas_call(
        _conv_inproj_kernel,
        grid=(t // tm,),
        in_specs=[row, _const_spec(w.shape)],
        out_specs=[row, row],
        out_shape=[jax.ShapeDtypeStruct((t, D_MODEL), BF16)] * 2,
        compiler_params=_params("arbitrary"),
        name="conv_inproj",
    )(x, w)


def _conv_post_kernel(tiles_per_seq, u_ref, up_ref, un_ref, bg_ref, x_ref, cw_ref, wo_ref,
                      g_ref, b_ref, rw2_ref, rb_ref, xa_ref, route_ref, cnt_ref):
    tm = u_ref.shape[0]
    pos = pl.program_id(0) % tiles_per_seq
    u = u_ref[...].astype(F32)
    prev_row = jnp.where(pos == 0, 0.0, up_ref[BF16_SUBLANES - 1:BF16_SUBLANES, :].astype(F32))
    next_row = jnp.where(pos == tiles_per_seq - 1, 0.0, un_ref[0:1, :].astype(F32))
    ri = lax.broadcasted_iota(I32, (tm, 1), 0)
    u_m1 = jnp.where(ri == 0, prev_row, pltpu.roll(u, 1, axis=0))
    u_p1 = jnp.where(ri == tm - 1, next_row, pltpu.roll(u, tm - 1, axis=0))
    cw = cw_ref[...]
    conv = cw[0:1, :] * u_m1 + cw[1:2, :] * u + cw[2:3, :] * u_p1
    z = (bg_ref[...].astype(F32) * conv).astype(BF16)
    _post_mixer_tail(z, x_ref[...], wo_ref, g_ref, b_ref, rw2_ref, rb_ref, xa_ref, route_ref, cnt_ref)


def _conv_post(u, bg, x, cw, wo, g, b, rw2, rb, seq):
    t = x.shape[0]
    tm = min(ROW_TILE, seq)
    halo_per_tile = tm // BF16_SUBLANES
    n_halo = t // BF16_SUBLANES
    row = lambda w: pl.BlockSpec((tm, w), lambda i: (i, 0))
    prev_spec = pl.BlockSpec((BF16_SUBLANES, D_MODEL),
                             lambda i: (jnp.maximum(i * halo_per_tile - 1, 0), 0))
    next_spec = pl.BlockSpec((BF16_SUBLANES, D_MODEL),
                             lambda i: (jnp.minimum((i + 1) * halo_per_tile, n_halo - 1), 0))
    out_specs, out_shape = _post_out(t, tm)
    return pl.pallas_call(
        functools.partial(_conv_post_kernel, seq // tm),
        grid=(t // tm,),
        in_specs=[row(D_MODEL), prev_spec, next_spec, row(D_MODEL), row(D_MODEL),
                  _const_spec(cw.shape), _const_spec(wo.shape), _const_spec(g.shape),
                  _const_spec(b.shape), _const_spec(rw2.shape), _const_spec(rb.shape)],
        out_specs=out_specs,
        out_shape=out_shape,
        compiler_params=_params("arbitrary"),
        name="conv_post",
    )(u, u, u, bg, x, cw, wo, g, b, rw2, rb)


def _prep_weights(router_w, router_b, mlstm_w_in, mlstm_b_gate, mlstm_norm_g, mlstm_w_out,
                  conv_w_in, conv_w, conv_w_out, ln1_g, ln1_b, moe_w_gu, moe_w_down, ln2_g, ln2_b):
    row = lambda v: v.reshape(1, -1).astype(F32)
    w_in = mlstm_w_in[0]
    col_scale = jnp.concatenate([jnp.ones((D_MODEL,), F32), jnp.full((D_MODEL,), HEAD_DIM ** -0.5, F32),
                                 jnp.ones((2 * D_MODEL,), F32)])
    w_qkvo = (w_in[:, :4 * D_MODEL] * col_scale).astype(BF16)
    pad = LANES - N_GATE_COLS
    w_gate = jnp.pad(w_in[:, 4 * D_MODEL:], ((0, 0), (0, pad))).astype(BF16)
    b_gate = jnp.pad(mlstm_b_gate[0], (0, pad)).reshape(1, LANES).astype(F32)
    rw_hi = router_w.astype(BF16)
    rw_lo = (router_w - rw_hi.astype(F32)).astype(BF16)
    padr = LANES - N_EXPERTS
    rw2 = jnp.concatenate([jnp.pad(rw_hi, ((0, 0), (0, padr))), jnp.pad(rw_lo, ((0, 0), (0, padr)))], axis=1)
    rb = jnp.pad(router_b.astype(F32), (0, padr)).reshape(1, LANES)
    return dict(
        w_qkvo=w_qkvo, w_gate=w_gate, b_gate=b_gate, norm_g=row(mlstm_norm_g[0]),
        mlstm_w_out=mlstm_w_out[0].astype(BF16),
        conv_w_in=conv_w_in[0].astype(BF16), conv_w=conv_w[0].astype(F32),
        conv_w_out=conv_w_out[0].astype(BF16),
        rw2=rw2, rb=rb,
        ln1_g=[row(ln1_g[i]) for i in range(DEPTH)], ln1_b=[row(ln1_b[i]) for i in range(DEPTH)],
        ln2_g=[row(ln2_g[i]) for i in range(DEPTH)], ln2_b=[row(ln2_b[i]) for i in range(DEPTH)],
        wgu=[moe_w_gu[i].astype(BF16) for i in range(DEPTH)],
        wd=[moe_w_down[i].astype(BF16) for i in range(DEPTH)],
    )


def _trunk(x3, p):
    bsz, seq, _ = x3.shape
    x = x3.reshape(bsz * seq, D_MODEL)
    qkvo, gc, gr = _mlstm_inproj(x, p["w_qkvo"], p["w_gate"], p["b_gate"])
    y = _mlstm_core(qkvo, gc, gr, p["norm_g"], bsz, seq)
    xa, route, counts = _mlstm_post(y, x, p["mlstm_w_out"], p["ln1_g"][0], p["ln1_b"][0], p["rw2"], p["rb"])
    x = _moe(xa, route, counts, p["wgu"][0], p["wd"][0], p["ln2_g"][0], p["ln2_b"][0])
    bg, u = _conv_inproj(x, p["conv_w_in"])
    xa, route, counts = _conv_post(u, bg, x, p["conv_w"], p["conv_w_out"], p["ln1_g"][1], p["ln1_b"][1],
                                   p["rw2"], p["rb"], seq)
    x = _moe(xa, route, counts, p["wgu"][1], p["wd"][1], p["ln2_g"][1], p["ln2_b"][1])
    return x.reshape(bsz, seq, D_MODEL)


def kernel(x_prompt, x_sample, router_w, router_b, mlstm_w_in, mlstm_b_gate, mlstm_norm_g, mlstm_w_out,
           conv_w_in, conv_w, conv_w_out, ln1_g, ln1_b, moe_w_gu, moe_w_down, ln2_g, ln2_b):
    p = _prep_weights(router_w, router_b, mlstm_w_in, mlstm_b_gate, mlstm_norm_g, mlstm_w_out,
                      conv_w_in, conv_w, conv_w_out, ln1_g, ln1_b, moe_w_gu, moe_w_down, ln2_g, ln2_b)
    return (_trunk(x_prompt, p), _trunk(x_sample, p))
```

```python
import functools

import jax
import jax.numpy as jnp
import numpy as np
from jax import lax
from jax.experimental import pallas as pl
from jax.experimental.pallas import tpu as pltpu

F32 = jnp.float32
BF16 = jnp.bfloat16
I32 = jnp.int32

D_MODEL = 1024
DEPTH = 2
N_HEADS = 4
HEAD_DIM = D_MODEL // N_HEADS
CHUNK = 128
N_GATE_COLS = 4 * N_HEADS
M_INIT = -1e30
HEAD_NORM_EPS = 1e-6
CONV_W = 3
N_EXPERTS = 16
N_GROUPS = 4
EXPERTS_PER_GROUP = N_EXPERTS // N_GROUPS
D_EXPERT = 256
LN_EPS = 1e-5
ALPHA = (2.0 * DEPTH) ** 0.25

LANES = 128
SUBLANES = 8
BF16_SUBLANES = 16
VMEM_LIMIT = 56 * 1024 * 1024

ROW_TILE = 512
SORT_TILE = 256
HEADS_PER_STEP = 2
DMA_ROW_TILE = 1024
ROW_DMA_UNROLL = 8

_PAIRS = [(a, b) for a in range(EXPERTS_PER_GROUP) for b in range(a + 1, EXPERTS_PER_GROUP)]
N_CLASSES = N_GROUPS * len(_PAIRS)
CLASS_ROWS = 32
_CLASS_LO = np.array([EXPERTS_PER_GROUP * g + a for g in range(N_GROUPS) for a, _ in _PAIRS], np.int32)
_CLASS_HI = np.array([EXPERTS_PER_GROUP * g + b for g in range(N_GROUPS) for _, b in _PAIRS], np.int32)
AUG = D_MODEL + LANES


def _params(*sem, flags=None):
    return pltpu.CompilerParams(dimension_semantics=sem, vmem_limit_bytes=VMEM_LIMIT, flags=flags)


def _const_spec(shape):
    nd = len(shape)
    return pl.BlockSpec(shape, lambda *_: (0,) * nd)


def _dot(a, b):
    return jnp.dot(a, b, preferred_element_type=F32)


def _layer_norm(v, g, b):
    mu = jnp.mean(v, axis=-1, keepdims=True)
    d = v - mu
    var = jnp.mean(d * d, axis=-1, keepdims=True)
    return d * lax.rsqrt(var + LN_EPS) * g + b


def _split3(v):
    a = v.astype(BF16)
    r = v - a.astype(F32)
    b = r.astype(BF16)
    c = (r - b.astype(F32)).astype(BF16)
    return a, b, c


def _mlstm_inproj_kernel(x_ref, w_ref, wg_ref, bg_ref, qkvo_ref, gc_ref, gr_ref):
    tm = x_ref.shape[0]
    xb = x_ref[...].astype(BF16)
    for j in range(4):
        sl = slice(j * D_MODEL, (j + 1) * D_MODEL)
        qkvo_ref[:, sl] = _dot(xb, w_ref[:, sl]).astype(BF16)

    gates = _dot(xb, wg_ref[...]) + bg_ref[...]
    logf = jax.nn.log_sigmoid(gates)
    lane = lax.broadcasted_iota(I32, (CHUNK, LANES), 1)
    is_f_fw = (lane >= N_HEADS) & (lane < 2 * N_HEADS)
    is_f_bw = (lane >= 3 * N_HEADS) & (lane < 4 * N_HEADS)
    ri = lax.broadcasted_iota(I32, (CHUNK, CHUNK), 0)
    ci = lax.broadcasted_iota(I32, (CHUNK, CHUNK), 1)
    tri_lo = jnp.where(ci <= ri, 1.0, 0.0).astype(BF16)
    tri_up = jnp.where(ci >= ri, 1.0, 0.0).astype(BF16)
    for c in range(tm // CHUNK):
        rows = slice(c * CHUNK, (c + 1) * CHUNK)
        a, b, d = _split3(logf[rows])
        pre = _dot(tri_lo, a) + _dot(tri_lo, b) + _dot(tri_lo, d)
        suf = _dot(tri_up, a) + _dot(tri_up, b) + _dot(tri_up, d)
        g = jnp.where(is_f_fw, pre, jnp.where(is_f_bw, suf, gates[rows]))
        gc_ref[rows, :] = g
        gr_ref[:, rows] = g.T[:N_GATE_COLS, :]


def _mlstm_inproj(x, w, wg, bg):
    t = x.shape[0]
    tm = min(ROW_TILE, t)
    return pl.pallas_call(
        _mlstm_inproj_kernel,
        grid=(t // tm,),
        in_specs=[
            pl.BlockSpec((tm, D_MODEL), lambda i: (i, 0)),
            _const_spec(w.shape),
            _const_spec(wg.shape),
            _const_spec(bg.shape),
        ],
        out_specs=[
            pl.BlockSpec((tm, 4 * D_MODEL), lambda i: (i, 0)),
            pl.BlockSpec((tm, LANES), lambda i: (i, 0)),
            pl.BlockSpec((N_GATE_COLS, tm), lambda i: (0, i)),
        ],
        out_shape=[
            jax.ShapeDtypeStruct((t, 4 * D_MODEL), BF16),
            jax.ShapeDtypeStruct((t, LANES), F32),
            jax.ShapeDtypeStruct((N_GATE_COLS, t), F32),
        ],
        compiler_params=_params("arbitrary"),
        name="mlstm_inproj",
    )(x, w, wg, bg)


def _each(fn, *lists):
    return [fn(*args) for args in zip(*lists)]


def _mlstm_chunks(q, k, v, f_col, a_row, f_tot, causal, c_ref, m_prev):
    ones = jnp.ones((CHUNK, LANES), BF16)
    v1 = _each(lambda x: jnp.concatenate([x, ones], axis=1), v)
    qk = _each(lambda a, b: lax.dot_general(a, b, (((1,), (1,)), ((), ())), preferred_element_type=F32), q, k)
    kt = _each(lambda x: x.T, k)
    c_prev = _each(lambda r: r[...], c_ref)
    qc = _each(lambda a, c: _dot(a, c.astype(BF16)), q, c_prev)
    d_log = _each(lambda m, f, a: jnp.where(m, f + a, -jnp.inf), causal, f_col, a_row)
    inter = _each(lambda f, m: f + m, f_col, m_prev)
    m_row = _each(lambda d, i: jnp.maximum(jnp.max(d, axis=1, keepdims=True), i), d_log, inter)
    m_new = _each(lambda ft, m, a: jnp.maximum(ft + m, jnp.max(ft + a, axis=1, keepdims=True)),
                  f_tot, m_prev, a_row)
    s = _each(lambda x, d, m: x * jnp.exp(d - m), qk, d_log, m_row)
    wk = _each(lambda ft, a, m: jnp.exp(ft + a - m), f_tot, a_row, m_new)
    kwt = _each(lambda x, w: (x.astype(F32) * w).astype(BF16), kt, wk)
    sv = _each(lambda x, y: _dot(x.astype(BF16), y), s, v1)
    kv = _each(_dot, kwt, v1)
    a_inter = _each(lambda i, m: jnp.exp(i - m), inter, m_row)
    decay = _each(lambda ft, m, mn: jnp.exp(ft + m - mn), f_tot, m_prev, m_new)
    comb = _each(lambda x, a, y: x + a * y, sv, a_inter, qc)
    h = _each(lambda x, m: x[:, :HEAD_DIM] / jnp.maximum(jnp.abs(x[:, HEAD_DIM:HEAD_DIM + 1]), jnp.exp(-m)),
              comb, m_row)
    for r, d, c, u in zip(c_ref, decay, c_prev, kv):
        r[...] = d * c + u
    return h, m_new


def _mlstm_core_kernel(q_ref, k_ref, v_ref, o_ref, gc_ref, gr_ref, ng_ref, y_ref, h_scr, c_scr):
    seq = q_ref.shape[0]
    nc = seq // CHUNK
    head0 = pl.program_id(1) * HEADS_PER_STEP
    lane = lax.broadcasted_iota(I32, (CHUNK, LANES), 1)
    sub = lax.broadcasted_iota(I32, (N_GATE_COLS, CHUNK), 0)
    ri = lax.broadcasted_iota(I32, (CHUNK, CHUNK), 0)
    ci = lax.broadcasted_iota(I32, (CHUNK, CHUNK), 1)
    chains = [(hh, d) for hh in range(HEADS_PER_STEP) for d in range(2)]

    def gate_col(gcc, idx):
        return jnp.sum(jnp.where(lane == idx, gcc, 0.0), axis=1, keepdims=True)

    def gate_row(grc, idx):
        return jnp.sum(jnp.where(sub == idx, grc, 0.0), axis=0, keepdims=True)

    tril, triu = ci <= ri, ci >= ri

    def step(finalize, t, carry):
        base = [2 * N_HEADS * d + head0 + hh for hh, d in chains]
        rows = [pl.ds(pl.multiple_of((t if d == 0 else nc - 1 - t) * CHUNK, CHUNK), CHUNK) for _, d in chains]
        cols = [slice(hh * HEAD_DIM, (hh + 1) * HEAD_DIM) for hh, _ in chains]
        gcc = _each(lambda r: gc_ref[r, :], rows)
        grc = _each(lambda r: gr_ref[:, r], rows)
        f_col = _each(lambda g, b: gate_col(g, b + N_HEADS), gcc, base)
        i_row = _each(gate_row, grc, base)
        f_row = _each(lambda g, b: gate_row(g, b + N_HEADS), grc, base)
        f_tot = [f[:, CHUNK - 1:CHUNK] if d == 0 else f[:, 0:1] for f, (_, d) in zip(f_row, chains)]
        h, m_new = _mlstm_chunks(
            _each(lambda r, c: q_ref[r, c], rows, cols), _each(lambda r, c: k_ref[r, c], rows, cols),
            _each(lambda r, c: v_ref[r, c], rows, cols),
            f_col, _each(lambda i, f: i - f, i_row, f_row),
            f_tot, [tril if d == 0 else triu for _, d in chains],
            [c_scr.at[hh, d] for hh, d in chains], list(carry))
        if not finalize:
            for r, c, x in zip(rows, cols, h):
                h_scr[r, c] = x
        else:
            hsum = _each(lambda r, c, x: h_scr[r, c] + x, rows, cols, h)
            dev = _each(lambda x: x - jnp.mean(x, axis=1, keepdims=True), hsum)
            var = _each(lambda x: jnp.mean(x * x, axis=1, keepdims=True), dev)
            hn = _each(lambda x, s, c: x * lax.rsqrt(s + HEAD_NORM_EPS) * ng_ref[:, c], dev, var, cols)
            for r, c, x in zip(rows, cols, hn):
                y_ref[r, c] = (jax.nn.sigmoid(o_ref[r, c].astype(F32)) * x).astype(BF16)
        return tuple(m_new)

    c_scr[...] = jnp.zeros_like(c_scr)
    init = tuple(jnp.full((1, 1), M_INIT, F32) for _ in chains)
    mid = lax.fori_loop(0, nc // 2, functools.partial(step, False), init)
    lax.fori_loop(nc // 2, nc, functools.partial(step, True), mid)


def _mlstm_core(qkvo, gc, gr, norm_g, bsz, seq):
    assert (seq // CHUNK) % 2 == 0
    width = HEADS_PER_STEP * HEAD_DIM
    steps = N_HEADS // HEADS_PER_STEP
    blk = lambda off: pl.BlockSpec((seq, width), lambda b, h: (b, off + h))
    return pl.pallas_call(
        _mlstm_core_kernel,
        grid=(bsz, steps),
        in_specs=[
            blk(0), blk(steps), blk(2 * steps), blk(3 * steps),
            pl.BlockSpec((seq, LANES), lambda b, h: (b, 0)),
            pl.BlockSpec((N_GATE_COLS, seq), lambda b, h: (0, b)),
            pl.BlockSpec((1, width), lambda b, h: (0, h)),
        ],
        out_specs=pl.BlockSpec((seq, width), lambda b, h: (b, h)),
        out_shape=jax.ShapeDtypeStruct((bsz * seq, D_MODEL), BF16),
        scratch_shapes=[pltpu.VMEM((seq, width), F32),
                        pltpu.VMEM((HEADS_PER_STEP, 2, HEAD_DIM, HEAD_DIM + LANES), F32)],
        compiler_params=_params("arbitrary", "arbitrary"),
        name="mlstm_core",
    )(qkvo, qkvo, qkvo, qkvo, gc, gr, norm_g)


def _route(x1, rw2_ref, rb_ref):
    tm = x1.shape[0]
    hi = x1.astype(BF16)
    lo = (x1 - hi.astype(F32)).astype(BF16)
    ph = _dot(hi, rw2_ref[...])
    pq = _dot(lo, rw2_ref[...])
    logits = ph[:, :LANES] + ph[:, LANES:] + pq[:, :LANES] + rb_ref[...]
    lt = logits.T[:N_EXPERTS, :]
    e = jnp.exp(lt - jnp.max(lt, axis=0, keepdims=True))
    p = e / jnp.sum(e, axis=0, keepdims=True)
    best = None
    for g in range(N_GROUPS):
        a, b, c, d = (p[EXPERTS_PER_GROUP * g + i:EXPERTS_PER_GROUP * g + i + 1, :]
                      for i in range(EXPERTS_PER_GROUP))
        hi1, lo1 = jnp.maximum(a, b), jnp.minimum(a, b)
        hi2, lo2 = jnp.maximum(c, d), jnp.minimum(c, d)
        score = jnp.maximum(hi1, hi2) + jnp.maximum(jnp.minimum(hi1, hi2), jnp.maximum(lo1, lo2))
        if best is None:
            best, g_sel = score, jnp.zeros((1, tm), I32)
        else:
            upd = score > best
            best = jnp.where(upd, score, best)
            g_sel = jnp.where(upd, g, g_sel)
    eid = lax.broadcasted_iota(I32, (N_EXPERTS, tm), 0)
    pm = jnp.where(eid // EXPERTS_PER_GROUP == g_sel, p, -1.0)
    v1 = jnp.max(pm, axis=0, keepdims=True)
    i1 = jnp.min(jnp.where(pm == v1, eid, N_EXPERTS), axis=0, keepdims=True)
    pm2 = jnp.where(eid == i1, -2.0, pm)
    v2 = jnp.max(pm2, axis=0, keepdims=True)
    i2 = jnp.min(jnp.where(pm2 == v2, eid, N_EXPERTS), axis=0, keepdims=True)
    tot = v1 + v2
    w1, w2 = v1 / tot, v2 / tot
    first_lo = i1 < i2
    e_lo, e_hi = jnp.where(first_lo, i1, i2), jnp.where(first_lo, i2, i1)
    c_lo, c_hi = jnp.where(first_lo, w1, w2), jnp.where(first_lo, w2, w1)
    a = e_lo - EXPERTS_PER_GROUP * g_sel
    b = e_hi - EXPERTS_PER_GROUP * g_sel
    pair = jnp.where(a == 0, 0, jnp.where(a == 1, 3, 5)) + (b - a - 1)
    cls = g_sel * len(_PAIRS) + pair

    cid = lax.broadcasted_iota(I32, (CLASS_ROWS, tm), 0)
    member = cid == cls
    onehot = jnp.where(member, 1.0, 0.0).astype(BF16)
    ri = lax.broadcasted_iota(I32, (tm, tm), 0)
    ci = lax.broadcasted_iota(I32, (tm, tm), 1)
    earlier = jnp.where(ri < ci, 1.0, 0.0).astype(BF16)
    prefix = _dot(onehot, earlier)
    rank = jnp.sum(jnp.where(member, prefix, 0.0), axis=0, keepdims=True).astype(I32)
    counts = _dot(onehot, jnp.ones((tm, LANES), BF16))

    sub = lax.broadcasted_iota(I32, (LANES, tm), 0)
    aux = jnp.where(sub == 0, c_lo, jnp.where(sub == 1, c_hi, 0.0)).T
    sub8 = lax.broadcasted_iota(I32, (SUBLANES, tm), 0)
    route = jnp.where(sub8 == 0, cls, jnp.where(sub8 == 1, rank, 0))
    return aux, route, counts


def _post_mixer_tail(z, x, wo_ref, g_ref, b_ref, rw2_ref, rb_ref, xa_ref, route_ref, cnt_ref):
    mix = _dot(z, wo_ref[...])
    x1 = _layer_norm(ALPHA * x + mix, g_ref[...], b_ref[...])
    aux, route, counts = _route(x1, rw2_ref, rb_ref)
    xa_ref[:, :D_MODEL] = x1
    xa_ref[:, D_MODEL:] = aux
    route_ref[...] = route
    cnt_ref[...] = counts


def _post_out(t, tm):
    specs = [pl.BlockSpec((tm, AUG), lambda i: (i, 0)),
             pl.BlockSpec((None, SUBLANES, tm), lambda i: (i, 0, 0)),
             pl.BlockSpec((None, CLASS_ROWS, LANES), lambda i: (i, 0, 0))]
    shapes = [jax.ShapeDtypeStruct((t, AUG), F32),
              jax.ShapeDtypeStruct((t // tm, SUBLANES, tm), I32),
              jax.ShapeDtypeStruct((t // tm, CLASS_ROWS, LANES), F32)]
    return specs, shapes


def _mlstm_post_kernel(y_ref, x_ref, wo_ref, g_ref, b_ref, rw2_ref, rb_ref, xa_ref, route_ref, cnt_ref):
    _post_mixer_tail(y_ref[...], x_ref[...], wo_ref, g_ref, b_ref, rw2_ref, rb_ref,
                     xa_ref, route_ref, cnt_ref)


def _mlstm_post(y, x, wo, g, b, rw2, rb):
    t = x.shape[0]
    tm = min(ROW_TILE, t)
    row = lambda w: pl.BlockSpec((tm, w), lambda i: (i, 0))
    out_specs, out_shape = _post_out(t, tm)
    return pl.pallas_call(
        _mlstm_post_kernel,
        grid=(t // tm,),
        in_specs=[row(D_MODEL), row(D_MODEL), _const_spec(wo.shape), _const_spec(g.shape),
                  _const_spec(b.shape), _const_spec(rw2.shape), _const_spec(rb.shape)],
        out_specs=out_specs,
        out_shape=out_shape,
        compiler_params=_params("arbitrary"),
        name="mlstm_post",
    )(y, x, wo, g, b, rw2, rb)


def _moe_plan(route, counts, tm):
    cls, rank = route[:, 0, :], route[:, 1, :]
    cnt = counts[:, :, 0].astype(I32)
    n_tiles = cls.shape[0]
    t = n_tiles * tm
    n_sorted = (t + N_CLASSES * SORT_TILE) // SORT_TILE
    tile_prefix = jnp.cumsum(cnt, axis=0) - cnt
    totals = jnp.sum(cnt, axis=0)
    padded = (totals + SORT_TILE - 1) // SORT_TILE * SORT_TILE
    class_end = jnp.cumsum(padded)
    class_base = class_end - padded
    off = class_base[None, :] + tile_prefix
    onehot = cls[:, :, None] == jnp.arange(CLASS_ROWS, dtype=I32)
    slot = rank + jnp.sum(jnp.where(onehot, off[:, None, :], 0), axis=-1)
    starts = jnp.arange(n_sorted, dtype=I32) * SORT_TILE
    tile_cls = jnp.sum((starts[:, None] >= class_end[None, :N_CLASSES - 1]).astype(I32), axis=1)
    n_valid = jnp.clip(class_base[tile_cls] + totals[tile_cls] - starts, 0, SORT_TILE).astype(I32)
    tile_lo = jnp.asarray(_CLASS_LO)[tile_cls]
    tile_hi = jnp.asarray(_CLASS_HI)[tile_cls]
    dt = min(DMA_ROW_TILE, t)
    return slot.reshape(t // dt, 1, dt).astype(I32), tile_lo, tile_hi, n_valid, n_sorted


def _row_copy(src_ref, src_row, dst_ref, dst_row, sem):
    return pltpu.make_async_copy(src_ref.at[pl.ds(src_row, 1), :], dst_ref.at[pl.ds(dst_row, 1), :], sem)


def _start_row_copies(n_rows, copy_for_row):
    def body(i, carry):
        for u in range(ROW_DMA_UNROLL):
            copy_for_row(i * ROW_DMA_UNROLL + u).start(priority=u % 2)
        return carry

    lax.fori_loop(0, n_rows // ROW_DMA_UNROLL, body, 0)


def _moe_dispatch_kernel(slot_ref, x_ref, sorted_in_ref, sorted_ref, sem):
    del sorted_in_ref
    tm = x_ref.shape[0]
    _start_row_copies(tm, lambda r: _row_copy(x_ref, r, sorted_ref, slot_ref[0, r], sem))
    pltpu.make_async_copy(x_ref, sorted_ref.at[pl.ds(0, tm), :], sem).wait()


def _moe_dispatch(xa, slot, n_rows):
    t = xa.shape[0]
    tm = slot.shape[-1]
    sorted0 = jnp.zeros((n_rows, AUG), F32)
    return pl.pallas_call(
        _moe_dispatch_kernel,
        grid=(t // tm,),
        in_specs=[pl.BlockSpec((None, 1, tm), lambda i: (i, 0, 0), memory_space=pltpu.SMEM),
                  pl.BlockSpec((tm, AUG), lambda i: (i, 0)),
                  pl.BlockSpec(memory_space=pl.ANY)],
        out_specs=pl.BlockSpec(memory_space=pl.ANY),
        out_shape=jax.ShapeDtypeStruct((n_rows, AUG), F32),
        scratch_shapes=[pltpu.SemaphoreType.DMA],
        input_output_aliases={2: 0},
        compiler_params=_params("arbitrary"),
        name="moe_dispatch",
    )(slot, xa, sorted0)


def _moe_experts_kernel(lo_ref, hi_ref, nv_ref, xs_ref, wgl_ref, wgh_ref, wdl_ref, wdh_ref,
                        g_ref, b_ref, out_ref):
    del lo_ref, hi_ref
    occupied = nv_ref[pl.program_id(0)] > 0

    @pl.when(jnp.logical_not(occupied))
    def _():
        out_ref[...] = jnp.zeros_like(out_ref)

    @pl.when(occupied)
    def _():
        x = xs_ref[:, :D_MODEL]
        xb = x.astype(BF16)

        def hidden(wg_ref, c_col):
            gu = _dot(xb, wg_ref[...])
            return (jax.nn.silu(gu[:, :D_EXPERT]) * gu[:, D_EXPERT:] * c_col).astype(BF16)

        ffn = (_dot(hidden(wgl_ref, xs_ref[:, D_MODEL:D_MODEL + 1]), wdl_ref[...])
               + _dot(hidden(wgh_ref, xs_ref[:, D_MODEL + 1:D_MODEL + 2]), wdh_ref[...]))
        out_ref[...] = _layer_norm(ALPHA * x + ffn, g_ref[...], b_ref[...])


def _moe_experts(xs, tile_lo, tile_hi, n_valid, wgu, wd, g, b):
    n_rows = xs.shape[0]
    grid_spec = pltpu.PrefetchScalarGridSpec(
        num_scalar_prefetch=3,
        grid=(n_rows // SORT_TILE,),
        in_specs=[
            pl.BlockSpec((SORT_TILE, AUG), lambda j, lo, hi, nv: (j, 0)),
            pl.BlockSpec((None, D_MODEL, 2 * D_EXPERT), lambda j, lo, hi, nv: (lo[j], 0, 0)),
            pl.BlockSpec((None, D_MODEL, 2 * D_EXPERT), lambda j, lo, hi, nv: (hi[j], 0, 0)),
            pl.BlockSpec((None, D_EXPERT, D_MODEL), lambda j, lo, hi, nv: (lo[j], 0, 0)),
            pl.BlockSpec((None, D_EXPERT, D_MODEL), lambda j, lo, hi, nv: (hi[j], 0, 0)),
            pl.BlockSpec(g.shape, lambda j, lo, hi, nv: (0, 0)),
            pl.BlockSpec(b.shape, lambda j, lo, hi, nv: (0, 0)),
        ],
        out_specs=pl.BlockSpec((SORT_TILE, D_MODEL), lambda j, lo, hi, nv: (j, 0)),
    )
    return pl.pallas_call(
        _moe_experts_kernel,
        grid_spec=grid_spec,
        out_shape=jax.ShapeDtypeStruct((n_rows, D_MODEL), F32),
        compiler_params=_params("arbitrary"),
        name="moe_experts",
    )(tile_lo, tile_hi, n_valid, xs, wgu, wgu, wd, wd, g, b)


def _moe_return_kernel(slot_ref, sorted_ref, out_ref, sem):
    tm = out_ref.shape[0]
    _start_row_copies(tm, lambda r: _row_copy(sorted_ref, slot_ref[0, r], out_ref, r, sem))
    pltpu.make_async_copy(sorted_ref.at[pl.ds(0, tm), :], out_ref, sem).wait()


def _moe_return(ys, slot):
    n_tiles, _, tm = slot.shape
    t = n_tiles * tm
    return pl.pallas_call(
        _moe_return_kernel,
        grid=(n_tiles,),
        in_specs=[pl.BlockSpec((None, 1, tm), lambda i: (i, 0, 0), memory_space=pltpu.SMEM),
                  pl.BlockSpec(memory_space=pl.ANY)],
        out_specs=pl.BlockSpec((tm, D_MODEL), lambda i: (i, 0)),
        out_shape=jax.ShapeDtypeStruct((t, D_MODEL), F32),
        scratch_shapes=[pltpu.SemaphoreType.DMA],
        compiler_params=_params("arbitrary"),
        name="moe_return",
    )(slot, ys)


def _moe(xa, route, counts, wgu, wd, g, b):
    tm = route.shape[-1]
    slot, tile_lo, tile_hi, n_valid, n_sorted = _moe_plan(route, counts, tm)
    xs = _moe_dispatch(xa, slot, n_sorted * SORT_TILE)
    ys = _moe_experts(xs, tile_lo, tile_hi, n_valid, wgu, wd, g, b)
    return _moe_return(ys, slot)


def _conv_inproj_kernel(x_ref, w_ref, bg_ref, u_ref):
    xb = x_ref[...].astype(BF16)
    bg_ref[...] = _dot(xb, w_ref[:, :D_MODEL]).astype(BF16)
    c_gate = _dot(xb, w_ref[:, D_MODEL:2 * D_MODEL])
    hv = _dot(xb, w_ref[:, 2 * D_MODEL:])
    u_ref[...] = (c_gate * hv).astype(BF16)


def _conv_inproj(x, w):
    t = x.shape[0]
    tm = min(ROW_TILE, t)
    row = pl.BlockSpec((tm, D_MODEL), lambda i: (i, 0))
    return pl.pallas_call(
        _conv_inproj_kernel,
        grid=(t // tm,),
        in_specs=[row, _const_spec(w.shape)],
        out_specs=[row, row],
        out_shape=[jax.ShapeDtypeStruct((t, D_MODEL), BF16)] * 2,
        compiler_params=_params("arbitrary"),
        name="conv_inproj",
    )(x, w)


def _conv_post_kernel(tiles_per_seq, u_ref, up_ref, un_ref, bg_ref, x_ref, cw_ref, wo_ref,
                      g_ref, b_ref, rw2_ref, rb_ref, xa_ref, route_ref, cnt_ref):
    tm = u_ref.shape[0]
    pos = pl.program_id(0) % tiles_per_seq
    u = u_ref[...].astype(F32)
    prev_row = jnp.where(pos == 0, 0.0, up_ref[BF16_SUBLANES - 1:BF16_SUBLANES, :].astype(F32))
    next_row = jnp.where(pos == tiles_per_seq - 1, 0.0, un_ref[0:1, :].astype(F32))
    ri = lax.broadcasted_iota(I32, (tm, 1), 0)
    u_m1 = jnp.where(ri == 0, prev_row, pltpu.roll(u, 1, axis=0))
    u_p1 = jnp.where(ri == tm - 1, next_row, pltpu.roll(u, tm - 1, axis=0))
    cw = cw_ref[...]
    conv = cw[0:1, :] * u_m1 + cw[1:2, :] * u + cw[2:3, :] * u_p1
    z = (bg_ref[...].astype(F32) * conv).astype(BF16)
    _post_mixer_tail(z, x_ref[...], wo_ref, g_ref, b_ref, rw2_ref, rb_ref, xa_ref, route_ref, cnt_ref)


def _conv_post(u, bg, x, cw, wo, g, b, rw2, rb, seq):
    t = x.shape[0]
    tm = min(ROW_TILE, seq)
    halo_per_tile = tm // BF16_SUBLANES
    n_halo = t // BF16_SUBLANES
    row = lambda w: pl.BlockSpec((tm, w), lambda i: (i, 0))
    prev_spec = pl.BlockSpec((BF16_SUBLANES, D_MODEL),
                             lambda i: (jnp.maximum(i * halo_per_tile - 1, 0), 0))
    next_spec = pl.BlockSpec((BF16_SUBLANES, D_MODEL),
                             lambda i: (jnp.minimum((i + 1) * halo_per_tile, n_halo - 1), 0))
    out_specs, out_shape = _post_out(t, tm)
    return pl.pallas_call(
        functools.partial(_conv_post_kernel, seq // tm),
        grid=(t // tm,),
        in_specs=[row(D_MODEL), prev_spec, next_spec, row(D_MODEL), row(D_MODEL),
                  _const_spec(cw.shape), _const_spec(wo.shape), _const_spec(g.shape),
                  _const_spec(b.shape), _const_spec(rw2.shape), _const_spec(rb.shape)],
        out_specs=out_specs,
        out_shape=out_shape,
        compiler_params=_params("arbitrary"),
        name="conv_post",
    )(u, u, u, bg, x, cw, wo, g, b, rw2, rb)


def _prep_weights(router_w, router_b, mlstm_w_in, mlstm_b_gate, mlstm_norm_g, mlstm_w_out,
                  conv_w_in, conv_w, conv_w_out, ln1_g, ln1_b, moe_w_gu, moe_w_down, ln2_g, ln2_b):
    row = lambda v: v.reshape(1, -1).astype(F32)
    w_in = mlstm_w_in[0]
    col_scale = jnp.concatenate([jnp.ones((D_MODEL,), F32), jnp.full((D_MODEL,), HEAD_DIM ** -0.5, F32),
                                 jnp.ones((2 * D_MODEL,), F32)])
    w_qkvo = (w_in[:, :4 * D_MODEL] * col_scale).astype(BF16)
    pad = LANES - N_GATE_COLS
    w_gate = jnp.pad(w_in[:, 4 * D_MODEL:], ((0, 0), (0, pad))).astype(BF16)
    b_gate = jnp.pad(mlstm_b_gate[0], (0, pad)).reshape(1, LANES).astype(F32)
    rw_hi = router_w.astype(BF16)
    rw_lo = (router_w - rw_hi.astype(F32)).astype(BF16)
    padr = LANES - N_EXPERTS
    rw2 = jnp.concatenate([jnp.pad(rw_hi, ((0, 0), (0, padr))), jnp.pad(rw_lo, ((0, 0), (0, padr)))], axis=1)
    rb = jnp.pad(router_b.astype(F32), (0, padr)).reshape(1, LANES)
    return dict(
        w_qkvo=w_qkvo, w_gate=w_gate, b_gate=b_gate, norm_g=row(mlstm_norm_g[0]),
        mlstm_w_out=mlstm_w_out[0].astype(BF16),
        conv_w_in=conv_w_in[0].astype(BF16), conv_w=conv_w[0].astype(F32),
        conv_w_out=conv_w_out[0].astype(BF16),
        rw2=rw2, rb=rb,
        ln1_g=[row(ln1_g[i]) for i in range(DEPTH)], ln1_b=[row(ln1_b[i]) for i in range(DEPTH)],
        ln2_g=[row(ln2_g[i]) for i in range(DEPTH)], ln2_b=[row(ln2_b[i]) for i in range(DEPTH)],
        wgu=[moe_w_gu[i].astype(BF16) for i in range(DEPTH)],
        wd=[moe_w_down[i].astype(BF16) for i in range(DEPTH)],
    )


def _trunk(x3, p):
    bsz, seq, _ = x3.shape
    x = x3.reshape(bsz * seq, D_MODEL)
    qkvo, gc, gr = _mlstm_inproj(x, p["w_qkvo"], p["w_gate"], p["b_gate"])
    y = _mlstm_core(qkvo, gc, gr, p["norm_g"], bsz, seq)
    xa, route, counts = _mlstm_post(y, x, p["mlstm_w_out"], p["ln1_g"][0], p["ln1_b"][0], p["rw2"], p["rb"])
    x = _moe(xa, route, counts, p["wgu"][0], p["wd"][0], p["ln2_g"][0], p["ln2_b"][0])
    bg, u = _conv_inproj(x, p["conv_w_in"])
    xa, route, counts = _conv_post(u, bg, x, p["conv_w"], p["conv_w_out"], p["ln1_g"][1], p["ln1_b"][1],
                                   p["rw2"], p["rb"], seq)
    x = _moe(xa, route, counts, p["wgu"][1], p["wd"][1], p["ln2_g"][1], p["ln2_b"][1])
    return x.reshape(bsz, seq, D_MODEL)


def kernel(x_prompt, x_sample, router_w, router_b, mlstm_w_in, mlstm_b_gate, mlstm_norm_g, mlstm_w_out,
           conv_w_in, conv_w, conv_w_out, ln1_g, ln1_b, moe_w_gu, moe_w_down, ln2_g, ln2_b):
    p = _prep_weights(router_w, router_b, mlstm_w_in, mlstm_b_gate, mlstm_norm_g, mlstm_w_out,
                      conv_w_in, conv_w, conv_w_out, ln1_g, ln1_b, moe_w_gu, moe_w_down, ln2_g, ln2_b)
    return (_trunk(x_prompt, p), _trunk(x_sample, p))
```
